```python
import jax, jax.numpy as jnp
from jax import lax
import numpy as np

D_MODEL = 1024
BATCH = 8
SEQ = 2048
DEPTH = 1

CHUNK = 64
CONV_WIDTH = 31
CONV_DIM = D_MODEL
RWKV_DIM = D_MODEL
RWKV_HEAD = 64
RWKV_HEADS = RWKV_DIM // RWKV_HEAD
DECAY_LORA = 64
AAA_LORA = 64
GATE_LORA = 128
D_FF = 4 * D_MODEL
N_BRANCH = 2
RMS_EPS = 1e-6
LN_EPS = 1e-5
GN_EPS = 64e-5
RWKV_COLS = 3 * RWKV_DIM + DECAY_LORA + AAA_LORA + GATE_LORA
IN_COLS = 2 * CONV_DIM + RWKV_COLS + N_BRANCH * D_MODEL

kernel_name = 'hybrid_conformer_rwkv7_adaln_block'


def rms_norm(x, gain):
    xf = x.astype(jnp.float32)
    y = xf * lax.rsqrt(jnp.mean(xf * xf, axis=-1, keepdims=True) + RMS_EPS)
    return (y * gain.astype(jnp.float32)).astype(x.dtype)


def layer_norm(x, gain, bias):
    xf = x.astype(jnp.float32)
    mu = jnp.mean(xf, axis=-1, keepdims=True)
    var = jnp.mean(jnp.square(xf - mu), axis=-1, keepdims=True)
    y = (xf - mu) * lax.rsqrt(var + LN_EPS)
    return (y * gain.astype(jnp.float32) + bias.astype(jnp.float32)).astype(x.dtype)


def modulate(h, shift, scale):
    return h * (1.0 + scale[:, None, :]) + shift[:, None, :]


def conformer_conv(u, conv_w, conv_b, ln_g, ln_b, w_pw, b_pw):
    za, zb = jnp.split(u, 2, axis=-1)
    z = za * jax.nn.sigmoid(zb)
    z = lax.conv_general_dilated(
        z, conv_w[:, None, :].astype(z.dtype), window_strides=(1,),
        padding=[(CONV_WIDTH - 1, 0)],
        dimension_numbers=('NWC', 'WIO', 'NWC'),
        feature_group_count=CONV_DIM) + conv_b
    z = jax.nn.silu(layer_norm(z, ln_g, ln_b))
    return z @ w_pw + b_pw


def rwkv7_scan(r, w, k, v, a, b):
    bsz, _, h, n = r.shape

    def step(S, inp):
        r_t, w_t, k_t, v_t, a_t, b_t = inp
        Sa = jnp.einsum('bhij,bhj->bhi', S, a_t)
        S = S * w_t[:, :, None, :] + Sa[..., None] * b_t[:, :, None, :] + v_t[..., None] * k_t[:, :, None, :]
        y = jnp.einsum('bhij,bhj->bhi', S, r_t)
        return S, y

    xs = tuple(jnp.moveaxis(t, 1, 0) for t in (r, w, k, v, a, b))
    S0 = jnp.zeros((bsz, h, n, n), jnp.float32)
    _, ys = lax.scan(step, S0, xs)
    return jnp.moveaxis(ys, 0, 1)


def rwkv7_mix(u, mix, w0, w2, a0, a2, g2, k_k, k_a, r_k, gn_g, gn_b, w_o):
    bsz, t, _ = u.shape
    u_prev = jnp.pad(u, ((0, 0), (1, 0), (0, 0)))[:, :-1]
    u = u + mix * (u_prev - u)
    r, k, v, xw, xa, xg = jnp.split(
        u, [RWKV_DIM, 2 * RWKV_DIM, 3 * RWKV_DIM, 3 * RWKV_DIM + DECAY_LORA,
            3 * RWKV_DIM + DECAY_LORA + AAA_LORA], axis=-1)
    w_log = -jax.nn.softplus(-(w0 + jnp.tanh(xw) @ w2)) - 0.5
    decay = jnp.exp(-jnp.exp(w_log.astype(jnp.float32)))
    a = jax.nn.sigmoid(a0 + xa @ a2)
    g = jax.nn.sigmoid(xg) @ g2
    hs = (bsz, t, RWKV_HEADS, RWKV_HEAD)
    kk = (k * k_k).astype(jnp.float32).reshape(hs)
    kk = kk / jnp.maximum(jnp.linalg.norm(kk, axis=-1, keepdims=True), 1e-12)
    k = k * (1.0 + (a - 1.0) * k_a)
    r_h, k_h, v_h = r.reshape(hs), k.reshape(hs), v.reshape(hs)
    a_h = a.astype(jnp.float32).reshape(hs)
    y = rwkv7_scan(r_h.astype(jnp.float32), decay.reshape(hs), k_h.astype(jnp.float32),
                   v_h.astype(jnp.float32), -kk, kk * a_h)
    mu = jnp.mean(y, axis=-1, keepdims=True)
    var = jnp.mean(jnp.square(y - mu), axis=-1, keepdims=True)
    y = ((y - mu) * lax.rsqrt(var + GN_EPS)).reshape(bsz, t, RWKV_DIM)
    y = (y * gn_g.astype(jnp.float32) + gn_b.astype(jnp.float32)).astype(u.dtype)
    bonus = (jnp.sum(r_h * k_h * r_k, axis=-1, keepdims=True) * v_h).reshape(bsz, t, RWKV_DIM)
    return ((y + bonus) * g) @ w_o


def hybrid_layer(x, c, w_ada, b_ada, g_norm1, w_in, conv_w, conv_b, conv_ln_g, conv_ln_b,
                 w_conv_pw, b_conv_pw, rwkv_mix, rwkv_w0, rwkv_w2, rwkv_a0, rwkv_a2, rwkv_g2,
                 rwkv_k_k, rwkv_k_a, rwkv_r_k, rwkv_gn_g, rwkv_gn_b, w_rwkv_o, w_out,
                 g_norm2, w_ff1, w_ff2):
    mod = jax.nn.silu(c) @ w_ada + b_ada
    shift1, scale1, gate1, shift2, scale2, gate2 = jnp.split(mod, 6, axis=-1)
    h = modulate(rms_norm(x, g_norm1), shift1, scale1)
    u = h @ w_in
    u_conv, u_rwkv, u_gate = jnp.split(u, [2 * CONV_DIM, 2 * CONV_DIM + RWKV_COLS], axis=-1)
    y_conv = conformer_conv(u_conv, conv_w, conv_b, conv_ln_g, conv_ln_b, w_conv_pw, b_conv_pw)
    y_rwkv = rwkv7_mix(u_rwkv, rwkv_mix, rwkv_w0, rwkv_w2, rwkv_a0, rwkv_a2, rwkv_g2,
                       rwkv_k_k, rwkv_k_a, rwkv_r_k, rwkv_gn_g, rwkv_gn_b, w_rwkv_o)
    g_conv, g_rwkv = jnp.split(jax.nn.sigmoid(u_gate), N_BRANCH, axis=-1)
    merged = g_conv * y_conv + g_rwkv * y_rwkv
    x = x + gate1[:, None, :] * (merged @ w_out)
    h2 = modulate(rms_norm(x, g_norm2), shift2, scale2)
    ff = jnp.square(jax.nn.relu(h2 @ w_ff1)) @ w_ff2
    return x + gate2[:, None, :] * ff


def setup_inputs(seed: int = 0) -> dict:
    key = jax.random.key(seed)
    ks = iter(jax.random.split(key, 40))
    f32 = jnp.float32
    L, D = DEPTH, D_MODEL

    def nrm(shape, scale):
        return jax.random.normal(next(ks), shape, f32) * scale

    return {
        'x': nrm((BATCH, SEQ, D), 1.0),
        'c': nrm((BATCH, D), 1.0),
        'w_ada': nrm((L, D, 6 * D), 0.5 * D ** -0.5),
        'b_ada': nrm((L, 6 * D), 0.01),
        'g_norm1': 1.0 + nrm((L, D), 0.02),
        'w_in': nrm((L, D, IN_COLS), D ** -0.5),
        'conv_w': nrm((L, CONV_WIDTH, CONV_DIM), CONV_WIDTH ** -0.5),
        'conv_b': nrm((L, CONV_DIM), 0.01),
        'conv_ln_g': 1.0 + nrm((L, CONV_DIM), 0.02),
        'conv_ln_b': nrm((L, CONV_DIM), 0.01),
        'w_conv_pw': nrm((L, CONV_DIM, D), CONV_DIM ** -0.5),
        'b_conv_pw': nrm((L, D), 0.01),
        'rwkv_mix': jax.random.uniform(next(ks), (L, RWKV_COLS), f32),
        'rwkv_w0': jax.random.uniform(next(ks), (L, RWKV_DIM), f32, -6.0, 1.0),
        'rwkv_w2': nrm((L, DECAY_LORA, RWKV_DIM), 0.5 * DECAY_LORA ** -0.5),
        'rwkv_a0': nrm((L, RWKV_DIM), 0.5),
        'rwkv_a2': nrm((L, AAA_LORA, RWKV_DIM), AAA_LORA ** -0.5),
        'rwkv_g2': nrm((L, GATE_LORA, RWKV_DIM), GATE_LORA ** -0.5),
        'rwkv_k_k': 0.85 + nrm((L, RWKV_DIM), 0.05),
        'rwkv_k_a': 1.0 + nrm((L, RWKV_DIM), 0.05),
        'rwkv_r_k': nrm((L, RWKV_HEADS, RWKV_HEAD), 0.1),
        'rwkv_gn_g': 1.0 + nrm((L, RWKV_DIM), 0.02),
        'rwkv_gn_b': nrm((L, RWKV_DIM), 0.01),
        'w_rwkv_o': nrm((L, RWKV_DIM, D), RWKV_DIM ** -0.5),
        'w_out': nrm((L, D, D), D ** -0.5),
        'g_norm2': 1.0 + nrm((L, D), 0.02),
        'w_ff1': nrm((L, D, D_FF), D ** -0.5),
        'w_ff2': nrm((L, D_FF, D), D_FF ** -0.5),
        'g_final': 1.0 + nrm((D,), 0.02),
    }


def reference(x, c, w_ada, b_ada, g_norm1, w_in, conv_w, conv_b, conv_ln_g, conv_ln_b,
              w_conv_pw, b_conv_pw, rwkv_mix, rwkv_w0, rwkv_w2, rwkv_a0, rwkv_a2, rwkv_g2,
              rwkv_k_k, rwkv_k_a, rwkv_r_k, rwkv_gn_g, rwkv_gn_b, w_rwkv_o, w_out,
              g_norm2, w_ff1, w_ff2, g_final):
    for l in range(DEPTH):
        x = hybrid_layer(x, c, w_ada[l], b_ada[l], g_norm1[l], w_in[l], conv_w[l], conv_b[l],
                         conv_ln_g[l], conv_ln_b[l], w_conv_pw[l], b_conv_pw[l], rwkv_mix[l],
                         rwkv_w0[l], rwkv_w2[l], rwkv_a0[l], rwkv_a2[l], rwkv_g2[l], rwkv_k_k[l],
                         rwkv_k_a[l], rwkv_r_k[l], rwkv_gn_g[l], rwkv_gn_b[l], w_rwkv_o[l],
                         w_out[l], g_norm2[l], w_ff1[l], w_ff2[l])
    return rms_norm(x, g_final)
```

```python
import functools

import jax
import jax.numpy as jnp
from jax import lax
from jax.experimental import pallas as pl
from jax.experimental.pallas import tpu as pltpu

F32 = jnp.float32
BF16 = jnp.bfloat16

D_MODEL = 1024
CONV_WIDTH = 31
CONV_DIM = D_MODEL
RWKV_DIM = D_MODEL
RWKV_HEAD = 64
DECAY_LORA = 64
AAA_LORA = 64
GATE_LORA = 128
D_FF = 4 * D_MODEL
RMS_EPS = 1e-6
LN_EPS = 1e-5
GN_EPS = 64e-5
RWKV_COLS = 3 * RWKV_DIM + DECAY_LORA + AAA_LORA + GATE_LORA
IN_COLS = 2 * CONV_DIM + RWKV_COLS + 2 * D_MODEL

LANES = 128
SUBLANES = 8
CHUNK = 64
PAIR = 2 * RWKV_HEAD
N_PAIRS = RWKV_DIM // PAIR
HSUM_W = 256
CONV_HALO = 32
CONV_ROWS = 64

TM_IN = 256
TT_CONV = 256
TM_POST = 512
VMEM_LIMIT = 56 * 1024 * 1024


def _dot(a, b):
    return jnp.dot(a, b, preferred_element_type=F32)


def _split2(x):
    hi = x.astype(BF16)
    lo = (x - hi.astype(F32)).astype(BF16)
    return hi, lo


def _split3(x):
    hi = x.astype(BF16)
    r1 = x - hi.astype(F32)
    mid = r1.astype(BF16)
    lo = (r1 - mid.astype(F32)).astype(BF16)
    return hi, mid, lo


def _sigmoid(x):
    return jax.nn.sigmoid(x)


def _rms_norm(x, gain):
    ms = jnp.mean(x * x, axis=-1, keepdims=True)
    return x * lax.rsqrt(ms + RMS_EPS) * gain


def _adaln_kernel(c_ref, w_ref, b_ref, o_ref):
    cv = c_ref[...]
    s = cv * _sigmoid(cv)
    s_hi, s_lo = _split2(s)
    w_hi, w_lo = _split2(w_ref[...])
    o_ref[...] = _dot(s_hi, w_hi) + _dot(s_hi, w_lo) + _dot(s_lo, w_hi) + b_ref[...]


def _adaln(c, w_ada, b_ada):
    bsz, d = c.shape
    n = w_ada.shape[1]
    tn = D_MODEL
    return pl.pallas_call(
        _adaln_kernel,
        grid=(n // tn,),
        in_specs=[pl.BlockSpec((bsz, d), lambda j: (0, 0)),
                  pl.BlockSpec((d, tn), lambda j: (0, j)),
                  pl.BlockSpec((1, tn), lambda j: (0, j))],
        out_specs=pl.BlockSpec((bsz, tn), lambda j: (0, j)),
        out_shape=jax.ShapeDtypeStruct((bsz, n), F32),
        compiler_params=pltpu.CompilerParams(dimension_semantics=("arbitrary",)),
        name="adaln",
    )(c, w_ada, b_ada.reshape(1, n))


def _inproj_kernel(x_ref, mod_ref, g1_ref, w_ref, z_ref, ur_ref, sg_ref):
    x = x_ref[0]
    m = mod_ref[0]
    h = _rms_norm(x, g1_ref[...])
    h = h * (1.0 + m[1:2]) + m[0:1]
    hb = h.astype(BF16)
    za = _dot(hb, w_ref[:, 0:CONV_DIM])
    zb = _dot(hb, w_ref[:, CONV_DIM:2 * CONV_DIM])
    z_ref[0] = za * _sigmoid(zb)
    ur_ref[0] = _dot(hb, w_ref[:, 2 * CONV_DIM:2 * CONV_DIM + RWKV_COLS])
    sg_ref[0] = _sigmoid(_dot(hb, w_ref[:, 2 * CONV_DIM + RWKV_COLS:IN_COLS]))


def _inproj(x, mod, g_norm1, w_in_bf):
    bsz, t, d = x.shape
    tm = TM_IN
    row = lambda b, i: (b, i, 0)
    return pl.pallas_call(
        _inproj_kernel,
        grid=(bsz, t // tm),
        in_specs=[pl.BlockSpec((1, tm, d), row),
                  pl.BlockSpec((1, 6, d), lambda b, i: (b, 0, 0)),
                  pl.BlockSpec((1, d), lambda b, i: (0, 0)),
                  pl.BlockSpec((d, IN_COLS), lambda b, i: (0, 0), pipeline_mode=pl.Buffered(1))],
        out_specs=[pl.BlockSpec((1, tm, CONV_DIM), row),
                   pl.BlockSpec((1, tm, RWKV_COLS), row),
                   pl.BlockSpec((1, tm, 2 * d), row)],
        out_shape=[jax.ShapeDtypeStruct((bsz, t, CONV_DIM), F32),
                   jax.ShapeDtypeStruct((bsz, t, RWKV_COLS), F32),
                   jax.ShapeDtypeStruct((bsz, t, 2 * d), F32)],
        compiler_params=pltpu.CompilerParams(dimension_semantics=("arbitrary", "arbitrary"),
                                             vmem_limit_bytes=VMEM_LIMIT),
        name="inproj",
    )(x, mod, g_norm1.reshape(1, d), w_in_bf)


def _conv_kernel(z_ref, sg_ref, cw_ref, cb_ref, lg_ref, lb_ref, wpw_ref, bpw_ref, o_ref, zbuf, cbuf):
    tt = z_ref.shape[1]
    ti = pl.program_id(1)

    @pl.when(ti == 0)
    def _():
        zbuf[0, 0:CONV_HALO, :] = jnp.zeros((CONV_HALO, CONV_DIM), F32)

    @pl.when(ti > 0)
    def _():
        zbuf[0, 0:CONV_HALO, :] = zbuf[0, tt:tt + CONV_HALO, :]

    zbuf[0, CONV_HALO:CONV_HALO + tt, :] = z_ref[0]
    shifted_rows = tt + CONV_HALO - SUBLANES
    for s in range(1, SUBLANES):
        zbuf[s, 0:shifted_rows, :] = zbuf[0, s:s + shifted_rows, :]

    first = CONV_HALO - (CONV_WIDTH - 1)
    for cb in range(CONV_DIM // LANES):
        cols = slice(cb * LANES, (cb + 1) * LANES)

        def body(i, carry, cols=cols):
            r0 = pl.multiple_of(i * CONV_ROWS, CONV_ROWS)
            acc = jnp.broadcast_to(cb_ref[:, cols], (CONV_ROWS, LANES))
            for k in range(CONV_WIDTH):
                s, a = (first + k) % SUBLANES, (first + k) // SUBLANES
                acc = acc + cw_ref[k:k + 1, cols] * zbuf[s, pl.ds(r0 + a * SUBLANES, CONV_ROWS), cols]
            cbuf[pl.ds(r0, CONV_ROWS), cols] = acc
            return carry

        lax.fori_loop(0, tt // CONV_ROWS, body, 0)

    cv = cbuf[...]
    mu = jnp.mean(cv, axis=-1, keepdims=True)
    dv = cv - mu
    var = jnp.mean(dv * dv, axis=-1, keepdims=True)
    yl = dv * lax.rsqrt(var + LN_EPS) * lg_ref[...] + lb_ref[...]
    sw = yl * _sigmoid(yl)
    y = _dot(sw.astype(BF16), wpw_ref[...]) + bpw_ref[...]
    o_ref[0] = sg_ref[0] * y


def _conv_branch(z, sg, conv_w, conv_b, ln_g, ln_b, w_pw_bf, b_pw):
    bsz, t, cd = z.shape
    tt = TT_CONV
    row = lambda b, i: (b, i, 0)
    vec = lambda b, i: (0, 0)
    return pl.pallas_call(
        _conv_kernel,
        grid=(bsz, t // tt),
        in_specs=[pl.BlockSpec((1, tt, cd), row),
                  pl.BlockSpec((1, tt, D_MODEL), row),
                  pl.BlockSpec((CONV_WIDTH, cd), vec),
                  pl.BlockSpec((1, cd), vec),
                  pl.BlockSpec((1, cd), vec),
                  pl.BlockSpec((1, cd), vec),
                  pl.BlockSpec((cd, D_MODEL), vec),
                  pl.BlockSpec((1, D_MODEL), vec)],
        out_specs=pl.BlockSpec((1, tt, D_MODEL), row),
        out_shape=jax.ShapeDtypeStruct((bsz, t, D_MODEL), F32),
        scratch_shapes=[pltpu.VMEM((SUBLANES, tt + CONV_HALO, cd), F32),
                        pltpu.VMEM((tt, cd), F32)],
        compiler_params=pltpu.CompilerParams(dimension_semantics=("arbitrary", "arbitrary"),
                                             vmem_limit_bytes=VMEM_LIMIT),
        name="conv_branch",
    )(z, sg, conv_w, conv_b.reshape(1, cd), ln_g.reshape(1, cd), ln_b.reshape(1, cd), w_pw_bf,
      b_pw.reshape(1, D_MODEL))


def _head_sum(x, hsum):
    outs = []
    for q in range(x.shape[1] // HSUM_W):
        hi, lo = _split2(x[:, q * HSUM_W:(q + 1) * HSUM_W])
        outs.append(_dot(hi, hsum) + _dot(lo, hsum))
    return jnp.concatenate(outs, axis=1)


def _block_diag(xp, first_head):
    zero = jnp.zeros_like(xp)
    return jnp.concatenate([jnp.where(first_head, xp, zero), jnp.where(first_head, zero, xp)], axis=0)


def _rwkv_kernel(ur_ref, mix_ref, w0_ref, w2_ref, a0_ref, a2_ref, g2_ref, kk_ref, ka_ref, rk_ref,
                 gg_ref, gb_ref, hsum_ref, ltri_ref, o_ref, ubuf, carry, zst):
    ci = pl.program_id(1)

    @pl.when(ci == 0)
    def _():
        carry[...] = jnp.zeros_like(carry)
        zst[...] = jnp.zeros_like(zst)

    u = ur_ref[0]
    ubuf[0:SUBLANES, :] = carry[...]
    ubuf[SUBLANES:SUBLANES + CHUNK, :] = u
    carry[...] = u[CHUNK - SUBLANES:CHUNK, :]
    prev = ubuf[SUBLANES - 1:SUBLANES - 1 + CHUNK, :]
    xs = u + mix_ref[...] * (prev - u)

    r = xs[:, 0:RWKV_DIM]
    k = xs[:, RWKV_DIM:2 * RWKV_DIM]
    v = xs[:, 2 * RWKV_DIM:3 * RWKV_DIM]
    xwa = xs[:, 3 * RWKV_DIM:3 * RWKV_DIM + DECAY_LORA + AAA_LORA]
    xg = xs[:, 3 * RWKV_DIM + DECAY_LORA + AAA_LORA:RWKV_COLS]

    q = w0_ref[...] + _dot(jnp.tanh(xwa).astype(BF16), w2_ref[...])
    w_log = -(jnp.maximum(-q, 0.0) + jnp.log1p(jnp.exp(-jnp.abs(q)))) - 0.5
    lw = -jnp.exp(w_log)
    asig = _sigmoid(a0_ref[...] + _dot(xwa.astype(BF16), a2_ref[...]))
    g = _dot(_sigmoid(xg).astype(BF16), g2_ref[...])

    hsum = hsum_ref[...]
    kraw = k * kk_ref[...]
    kk = kraw / jnp.maximum(jnp.sqrt(_head_sum(kraw * kraw, hsum)), 1e-12)
    k2 = k * (1.0 + (asig - 1.0) * ka_ref[...])
    bonus = _head_sum(r * k2 * rk_ref[...], hsum) * v

    ltri = ltri_ref[...]
    l_hi, l_mid, l_lo = _split3(lw)
    cl = _dot(ltri, l_hi) + _dot(ltri, l_mid) + _dot(ltri, l_lo)
    e_in = jnp.exp(cl)
    e_ex = jnp.exp(cl - lw)
    e_neg = jnp.exp(-cl)
    pc = e_in[CHUNK - 1:CHUNK, :]

    rt = (r * e_in).astype(BF16)
    at = (-kk * e_ex).astype(BF16)
    bt_f = kk * asig * e_neg
    kt_f = k2 * e_neg
    bt = bt_f.astype(BF16)
    kt = kt_f.astype(BF16)
    bh = (bt_f * pc).astype(BF16)
    kh = (kt_f * pc).astype(BF16)
    vb = v.astype(BF16)

    n2 = 2 * CHUNK
    lane = lax.broadcasted_iota(jnp.int32, (CHUNK, PAIR), 1)
    first_head = lane < RWKV_HEAD
    rows = lax.broadcasted_iota(jnp.int32, (n2, n2), 0)
    cols_i = lax.broadcasted_iota(jnp.int32, (n2, n2), 1)
    strict = rows > cols_i
    incl = rows >= cols_i
    eye = (rows == cols_i).astype(F32)
    nt = (((1,), (1,)), ((), ()))
    tn = (((0,), (0,)), ((), ()))

    ys = []
    for p in range(N_PAIRS):
        cs = slice(p * PAIR, (p + 1) * PAIR)
        bd_a = _block_diag(at[:, cs], first_head)
        bd_r = _block_diag(rt[:, cs], first_head)
        bd_b = _block_diag(bt[:, cs], first_head)
        bd_k = _block_diag(kt[:, cs], first_head)
        bd_bh = _block_diag(bh[:, cs], first_head)
        bd_kh = _block_diag(kh[:, cs], first_head)
        bd_v = _block_diag(vb[:, cs], first_head)

        gm = lax.dot_general(jnp.concatenate([bd_a, bd_r], axis=0), jnp.concatenate([bd_b, bd_k], axis=0),
                             nt, preferred_element_type=F32)
        a_ab = jnp.where(strict, gm[0:n2, 0:n2], 0.0)
        a_ak = jnp.where(strict, gm[0:n2, n2:2 * n2], 0.0)
        a_rb = jnp.where(incl, gm[n2:2 * n2, 0:n2], 0.0)
        a_rk = jnp.where(incl, gm[n2:2 * n2, n2:2 * n2], 0.0)

        tm = eye + a_ab
        pb = a_ab.astype(BF16)
        pw = _dot(pb, pb)
        for _ in range(4):
            pb = pw.astype(BF16)
            xm = _dot(pb, jnp.concatenate([tm.astype(BF16), pb], axis=1))
            tm = tm + xm[:, 0:n2]
            pw = xm[:, n2:2 * n2]
        tm = tm + _dot(pw.astype(BF16), tm.astype(BF16))

        ta = _dot(tm.astype(BF16), jnp.concatenate([bd_a, a_ak.astype(BF16)], axis=1))
        zp = zst[p]
        zv = jnp.concatenate([zp.astype(BF16), bd_v], axis=0)
        um = _dot(ta.astype(BF16), zv)
        ub = um.astype(BF16)
        ym = _dot(jnp.concatenate([bd_r, a_rk.astype(BF16)], axis=1), zv) + _dot(a_rb.astype(BF16), ub)
        pc_col = jnp.broadcast_to(pc[:, cs], (n2, n2)).T
        zst[p] = pc_col * zp + lax.dot_general(jnp.concatenate([bd_bh, bd_kh], axis=0),
                                               jnp.concatenate([ub, bd_v], axis=0),
                                               tn, preferred_element_type=F32)
        ys.append(ym[0:CHUNK, :] + ym[CHUNK:n2, :])

    y = jnp.concatenate(ys, axis=1)
    inv_n = 1.0 / RWKV_HEAD
    mu = _head_sum(y, hsum) * inv_n
    dy = y - mu
    var = _head_sum(dy * dy, hsum) * inv_n
    yn = dy * lax.rsqrt(var + GN_EPS) * gg_ref[...] + gb_ref[...]
    o_ref[0] = (yn + bonus) * g


def _rwkv_branch(ur, mix, w0, w2, a0, a2, g2, k_k, k_a, r_k, gn_g, gn_b):
    bsz, t, _ = ur.shape
    dm = RWKV_DIM
    lora_w = DECAY_LORA + AAA_LORA
    w2p = jnp.concatenate([w2, jnp.zeros((AAA_LORA, dm), F32)], axis=0).astype(BF16)
    a2p = jnp.concatenate([jnp.zeros((DECAY_LORA, dm), F32), a2], axis=0).astype(BF16)
    hid = jnp.arange(HSUM_W) // RWKV_HEAD
    hsum = (hid[:, None] == hid[None, :]).astype(BF16)
    tid = jnp.arange(CHUNK)
    ltri = (tid[:, None] >= tid[None, :]).astype(BF16)
    row = lambda b, i: (b, i, 0)
    vec = lambda b, i: (0, 0)
    vspec = pl.BlockSpec((1, dm), vec)
    return pl.pallas_call(
        _rwkv_kernel,
        grid=(bsz, t // CHUNK),
        in_specs=[pl.BlockSpec((1, CHUNK, RWKV_COLS), row),
                  pl.BlockSpec((1, RWKV_COLS), vec),
                  vspec,
                  pl.BlockSpec((lora_w, dm), vec),
                  vspec,
                  pl.BlockSpec((lora_w, dm), vec),
                  pl.BlockSpec((GATE_LORA, dm), vec),
                  vspec, vspec, vspec, vspec, vspec,
                  pl.BlockSpec((HSUM_W, HSUM_W), vec),
                  pl.BlockSpec((CHUNK, CHUNK), vec)],
        out_specs=pl.BlockSpec((1, CHUNK, dm), row),
        out_shape=jax.ShapeDtypeStruct((bsz, t, dm), F32),
        scratch_shapes=[pltpu.VMEM((SUBLANES + CHUNK, RWKV_COLS), F32),
                        pltpu.VMEM((SUBLANES, RWKV_COLS), F32),
                        pltpu.VMEM((N_PAIRS, 2 * CHUNK, PAIR), F32)],
        compiler_params=pltpu.CompilerParams(dimension_semantics=("arbitrary", "arbitrary"),
                                             vmem_limit_bytes=VMEM_LIMIT),
        name="rwkv_branch",
    )(ur, mix.reshape(1, RWKV_COLS), w0.reshape(1, dm), w2p, a0.reshape(1, dm), a2p, g2.astype(BF16),
      k_k.reshape(1, dm), k_a.reshape(1, dm), r_k.reshape(1, dm), gn_g.reshape(1, dm), gn_b.reshape(1, dm),
      hsum, ltri)


def _post_kernel(x_ref, yc_ref, yr_ref, sg_ref, mod_ref, wo_ref, wout_ref, g2_ref, w1_ref, w2_ref, gf_ref,
                 o_ref):
    m = mod_ref[0]
    y_rwkv = _dot(yr_ref[0].astype(BF16), wo_ref[...])
    merged = yc_ref[0] + sg_ref[0] * y_rwkv
    x1 = x_ref[0] + m[2:3] * _dot(merged.astype(BF16), wout_ref[...])
    h2 = _rms_norm(x1, g2_ref[...]) * (1.0 + m[4:5]) + m[3:4]
    hid = jnp.maximum(_dot(h2.astype(BF16), w1_ref[...]), 0.0)
    ff = _dot((hid * hid).astype(BF16), w2_ref[...])
    x2 = x1 + m[5:6] * ff
    o_ref[0] = _rms_norm(x2, gf_ref[...])


def _post(x, yc, yr, sg, mod, w_o_bf, w_out_bf, g_norm2, w1_bf, w2_bf, g_out):
    bsz, t, d = x.shape
    tm = TM_POST
    row = lambda b, i: (b, i, 0)
    vec = lambda b, i: (0, 0)
    once = pl.Buffered(1)
    return pl.pallas_call(
        _post_kernel,
        grid=(bsz, t // tm),
        in_specs=[pl.BlockSpec((1, tm, d), row),
                  pl.BlockSpec((1, tm, d), row),
                  pl.BlockSpec((1, tm, d), row),
                  pl.BlockSpec((1, tm, d), lambda b, i: (b, i, 1)),
                  pl.BlockSpec((1, 6, d), lambda b, i: (b, 0, 0)),
                  pl.BlockSpec((d, d), vec, pipeline_mode=once),
                  pl.BlockSpec((d, d), vec, pipeline_mode=once),
                  pl.BlockSpec((1, d), vec),
                  pl.BlockSpec((d, D_FF), vec, pipeline_mode=once),
                  pl.BlockSpec((D_FF, d), vec, pipeline_mode=once),
                  pl.BlockSpec((1, d), vec)],
        out_specs=pl.BlockSpec((1, tm, d), row),
        out_shape=jax.ShapeDtypeStruct((bsz, t, d), F32),
        compiler_params=pltpu.CompilerParams(dimension_semantics=("arbitrary", "arbitrary"),
                                             vmem_limit_bytes=VMEM_LIMIT),
        name="post_ffn",
    )(x, yc, yr, sg, mod, w_o_bf, w_out_bf, g_norm2.reshape(1, d), w1_bf, w2_bf, g_out.reshape(1, d))


def kernel(x, c, w_ada, b_ada, g_norm1, w_in, conv_w, conv_b, conv_ln_g, conv_ln_b, w_conv_pw, b_conv_pw,
           rwkv_mix, rwkv_w0, rwkv_w2, rwkv_a0, rwkv_a2, rwkv_g2, rwkv_k_k, rwkv_k_a, rwkv_r_k, rwkv_gn_g,
           rwkv_gn_b, w_rwkv_o, w_out, g_norm2, w_ff1, w_ff2, g_final):
    assert w_ada.shape[0] == 1, "this problem's DEPTH is 1"
    bsz = x.shape[0]
    l = 0
    mod = _adaln(c, w_ada[l], b_ada[l]).reshape(bsz, 6, D_MODEL)
    z, ur, sg = _inproj(x, mod, g_norm1[l], w_in[l].astype(BF16))
    yc = _conv_branch(z, sg, conv_w[l], conv_b[l], conv_ln_g[l], conv_ln_b[l],
                      w_conv_pw[l].astype(BF16), b_conv_pw[l])
    yr = _rwkv_branch(ur, rwkv_mix[l], rwkv_w0[l], rwkv_w2[l], rwkv_a0[l], rwkv_a2[l], rwkv_g2[l],
                      rwkv_k_k[l], rwkv_k_a[l], rwkv_r_k[l], rwkv_gn_g[l], rwkv_gn_b[l])
    return _post(x, yc, yr, sg, mod, w_rwkv_o[l].astype(BF16), w_out[l].astype(BF16), g_norm2[l],
                 w_ff1[l].astype(BF16), w_ff2[l].astype(BF16), g_final)
```

```python
import functools

import jax
import jax.numpy as jnp
from jax import lax
from jax.experimental import pallas as pl
from jax.experimental.pallas import tpu as pltpu

F32 = jnp.float32
BF16 = jnp.bfloat16

D_MODEL = 1024
CONV_WIDTH = 31
CONV_DIM = D_MODEL
RWKV_DIM = D_MODEL
RWKV_HEAD = 64
DECAY_LORA = 64
AAA_LORA = 64
GATE_LORA = 128
D_FF = 4 * D_MODEL
RMS_EPS = 1e-6
LN_EPS = 1e-5
GN_EPS = 64e-5
RWKV_COLS = 3 * RWKV_DIM + DECAY_LORA + AAA_LORA + GATE_LORA
IN_COLS = 2 * CONV_DIM + RWKV_COLS + 2 * D_MODEL

LANES = 128
SUBLANES = 8
CHUNK = 64
PAIR = 2 * RWKV_HEAD
N_PAIRS = RWKV_DIM // PAIR
HSUM_W = 256
CONV_HALO = 32
CONV_ROWS = 64

TM_IN = 256
TT_CONV = 256
TM_POST = 512
VMEM_LIMIT = 56 * 1024 * 1024


def _dot(a, b):
    return jnp.dot(a, b, preferred_element_type=F32)


def _split2(x):
    hi = x.astype(BF16)
    lo = (x - hi.astype(F32)).astype(BF16)
    return hi, lo


def _split3(x):
    hi = x.astype(BF16)
    r1 = x - hi.astype(F32)
    mid = r1.astype(BF16)
    lo = (r1 - mid.astype(F32)).astype(BF16)
    return hi, mid, lo


def _sigmoid(x):
    return jax.nn.sigmoid(x)


def _rms_norm(x, gain):
    ms = jnp.mean(x * x, axis=-1, keepdims=True)
    return x * lax.rsqrt(ms + RMS_EPS) * gain


def _adaln_kernel(c_ref, w_ref, b_ref, o_ref):
    cv = c_ref[...]
    s = cv * _sigmoid(cv)
    s_hi, s_lo = _split2(s)
    w_hi, w_lo = _split2(w_ref[...])
    o_ref[...] = _dot(s_hi, w_hi) + _dot(s_hi, w_lo) + _dot(s_lo, w_hi) + b_ref[...]


def _adaln(c, w_ada, b_ada):
    bsz, d = c.shape
    n = w_ada.shape[1]
    tn = D_MODEL
    return pl.pallas_call(
        _adaln_kernel,
        grid=(n // tn,),
        in_specs=[pl.BlockSpec((bsz, d), lambda j: (0, 0)),
                  pl.BlockSpec((d, tn), lambda j: (0, j)),
                  pl.BlockSpec((1, tn), lambda j: (0, j))],
        out_specs=pl.BlockSpec((bsz, tn), lambda j: (0, j)),
        out_shape=jax.ShapeDtypeStruct((bsz, n), F32),
        compiler_params=pltpu.CompilerParams(dimension_semantics=("arbitrary",)),
        name="adaln",
    )(c, w_ada, b_ada.reshape(1, n))


def _inproj_kernel(x_ref, mod_ref, g1_ref, w_ref, z_ref, ur_ref, sg_ref):
    x = x_ref[0]
    m = mod_ref[0]
    h = _rms_norm(x, g1_ref[...])
    h = h * (1.0 + m[1:2]) + m[0:1]
    hb = h.astype(BF16)
    za = _dot(hb, w_ref[:, 0:CONV_DIM])
    zb = _dot(hb, w_ref[:, CONV_DIM:2 * CONV_DIM])
    z_ref[0] = za * _sigmoid(zb)
    ur_ref[0] = _dot(hb, w_ref[:, 2 * CONV_DIM:2 * CONV_DIM + RWKV_COLS])
    sg_ref[0] = _sigmoid(_dot(hb, w_ref[:, 2 * CONV_DIM + RWKV_COLS:IN_COLS]))


def _inproj(x, mod, g_norm1, w_in_bf):
    bsz, t, d = x.shape
    tm = TM_IN
    row = lambda b, i: (b, i, 0)
    return pl.pallas_call(
        _inproj_kernel,
        grid=(bsz, t // tm),
        in_specs=[pl.BlockSpec((1, tm, d), row),
                  pl.BlockSpec((1, 6, d), lambda b, i: (b, 0, 0)),
                  pl.BlockSpec((1, d), lambda b, i: (0, 0)),
                  pl.BlockSpec((d, IN_COLS), lambda b, i: (0, 0), pipeline_mode=pl.Buffered(1))],
        out_specs=[pl.BlockSpec((1, tm, CONV_DIM), row),
                   pl.BlockSpec((1, tm, RWKV_COLS), row),
                   pl.BlockSpec((1, tm, 2 * d), row)],
        out_shape=[jax.ShapeDtypeStruct((bsz, t, CONV_DIM), F32),
                   jax.ShapeDtypeStruct((bsz, t, RWKV_COLS), F32),
                   jax.ShapeDtypeStruct((bsz, t, 2 * d), F32)],
        compiler_params=pltpu.CompilerParams(dimension_semantics=("arbitrary", "arbitrary"),
                                             vmem_limit_bytes=VMEM_LIMIT),
        name="inproj",
    )(x, mod, g_norm1.reshape(1, d), w_in_bf)


def _conv_kernel(z_ref, sg_ref, cw_ref, cb_ref, lg_ref, lb_ref, wpw_ref, bpw_ref, o_ref, zbuf, cbuf):
    tt = z_ref.shape[1]
    ti = pl.program_id(1)

    n_cb = CONV_DIM // LANES
    shifted_rows = tt + CONV_HALO - SUBLANES
    first = CONV_HALO - (CONV_WIDTH - 1)

    @pl.when(ti == 0)
    def _():
        zbuf[0, :, 0:CONV_HALO, :] = jnp.zeros((n_cb, CONV_HALO, LANES), F32)

    @pl.when(ti > 0)
    def _():
        zbuf[0, :, 0:CONV_HALO, :] = zbuf[0, :, tt:tt + CONV_HALO, :]

    for cb in range(n_cb):
        cols = slice(cb * LANES, (cb + 1) * LANES)
        zbuf[0, cb, CONV_HALO:CONV_HALO + tt, :] = z_ref[0, :, cols]
        for s in range(1, SUBLANES):
            zbuf[s, cb, 0:shifted_rows, :] = zbuf[0, cb, s:s + shifted_rows, :]

        def body(i, carry, cb=cb, cols=cols):
            r0 = pl.multiple_of(i * CONV_ROWS, CONV_ROWS)
            acc = jnp.broadcast_to(cb_ref[:, cols], (CONV_ROWS, LANES))
            for k in range(CONV_WIDTH):
                s, a = (first + k) % SUBLANES, (first + k) // SUBLANES
                acc = acc + cw_ref[k:k + 1, cols] * zbuf[s, cb, pl.ds(r0 + a * SUBLANES, CONV_ROWS), :]
            cbuf[pl.ds(r0, CONV_ROWS), cols] = acc
            return carry

        lax.fori_loop(0, tt // CONV_ROWS, body, 0)

    cv = cbuf[...]
    mu = jnp.mean(cv, axis=-1, keepdims=True)
    dv = cv - mu
    var = jnp.mean(dv * dv, axis=-1, keepdims=True)
    yl = dv * lax.rsqrt(var + LN_EPS) * lg_ref[...] + lb_ref[...]
    sw = yl * _sigmoid(yl)
    y = _dot(sw.astype(BF16), wpw_ref[...]) + bpw_ref[...]
    o_ref[0] = sg_ref[0] * y


def _conv_branch(z, sg, conv_w, conv_b, ln_g, ln_b, w_pw_bf, b_pw):
    bsz, t, cd = z.shape
    tt = TT_CONV
    row = lambda b, i: (b, i, 0)
    vec = lambda b, i: (0, 0)
    return pl.pallas_call(
        _conv_kernel,
        grid=(bsz, t // tt),
        in_specs=[pl.BlockSpec((1, tt, cd), row),
                  pl.BlockSpec((1, tt, D_MODEL), row),
                  pl.BlockSpec((CONV_WIDTH, cd), vec),
                  pl.BlockSpec((1, cd), vec),
                  pl.BlockSpec((1, cd), vec),
                  pl.BlockSpec((1, cd), vec),
                  pl.BlockSpec((cd, D_MODEL), vec),
                  pl.BlockSpec((1, D_MODEL), vec)],
        out_specs=pl.BlockSpec((1, tt, D_MODEL), row),
        out_shape=jax.ShapeDtypeStruct((bsz, t, D_MODEL), F32),
        scratch_shapes=[pltpu.VMEM((SUBLANES, cd // LANES, tt + CONV_HALO, LANES), F32),
                        pltpu.VMEM((tt, cd), F32)],
        compiler_params=pltpu.CompilerParams(dimension_semantics=("arbitrary", "arbitrary"),
                                             vmem_limit_bytes=VMEM_LIMIT),
        name="conv_branch",
    )(z, sg, conv_w, conv_b.reshape(1, cd), ln_g.reshape(1, cd), ln_b.reshape(1, cd), w_pw_bf,
      b_pw.reshape(1, D_MODEL))


def _head_sum(x, hsum):
    outs = []
    for q in range(x.shape[1] // HSUM_W):
        hi, lo = _split2(x[:, q * HSUM_W:(q + 1) * HSUM_W])
        outs.append(_dot(hi, hsum) + _dot(lo, hsum))
    return jnp.concatenate(outs, axis=1)


def _block_diag(xp, first_head):
    zero = jnp.zeros_like(xp)
    return jnp.concatenate([jnp.where(first_head, xp, zero), jnp.where(first_head, zero, xp)], axis=0)


def _rwkv_kernel(ur_ref, mix_ref, w0_ref, w2_ref, a0_ref, a2_ref, g2_ref, kk_ref, ka_ref, rk_ref,
                 gg_ref, gb_ref, hsum_ref, ltri_ref, o_ref, ubuf, carry, zst):
    ci = pl.program_id(1)

    @pl.when(ci == 0)
    def _():
        carry[...] = jnp.zeros_like(carry)
        zst[...] = jnp.zeros_like(zst)

    u = ur_ref[0]
    ubuf[0:SUBLANES, :] = carry[...]
    ubuf[SUBLANES:SUBLANES + CHUNK, :] = u
    carry[...] = u[CHUNK - SUBLANES:CHUNK, :]
    prev = ubuf[SUBLANES - 1:SUBLANES - 1 + CHUNK, :]
    xs = u + mix_ref[...] * (prev - u)

    r = xs[:, 0:RWKV_DIM]
    k = xs[:, RWKV_DIM:2 * RWKV_DIM]
    v = xs[:, 2 * RWKV_DIM:3 * RWKV_DIM]
    xwa = xs[:, 3 * RWKV_DIM:3 * RWKV_DIM + DECAY_LORA + AAA_LORA]
    xg = xs[:, 3 * RWKV_DIM + DECAY_LORA + AAA_LORA:RWKV_COLS]

    q = w0_ref[...] + _dot(jnp.tanh(xwa).astype(BF16), w2_ref[...])
    w_log = -(jnp.maximum(-q, 0.0) + jnp.log1p(jnp.exp(-jnp.abs(q)))) - 0.5
    lw = -jnp.exp(w_log)
    asig = _sigmoid(a0_ref[...] + _dot(xwa.astype(BF16), a2_ref[...]))
    g = _dot(_sigmoid(xg).astype(BF16), g2_ref[...])

    hsum = hsum_ref[...]
    kraw = k * kk_ref[...]
    kk = kraw / jnp.maximum(jnp.sqrt(_head_sum(kraw * kraw, hsum)), 1e-12)
    k2 = k * (1.0 + (asig - 1.0) * ka_ref[...])
    bonus = _head_sum(r * k2 * rk_ref[...], hsum) * v

    ltri = ltri_ref[...]
    l_hi, l_mid, l_lo = _split3(lw)
    cl = _dot(ltri, l_hi) + _dot(ltri, l_mid) + _dot(ltri, l_lo)
    e_in = jnp.exp(cl)
    e_ex = jnp.exp(cl - lw)
    e_neg = jnp.exp(-cl)
    pc = e_in[CHUNK - 1:CHUNK, :]

    rt = (r * e_in).astype(BF16)
    at = (-kk * e_ex).astype(BF16)
    bt_f = kk * asig * e_neg
    kt_f = k2 * e_neg
    bt = bt_f.astype(BF16)
    kt = kt_f.astype(BF16)
    bh = (bt_f * pc).astype(BF16)
    kh = (kt_f * pc).astype(BF16)
    vb = v.astype(BF16)

    n2 = 2 * CHUNK
    lane = lax.broadcasted_iota(jnp.int32, (CHUNK, PAIR), 1)
    first_head = lane < RWKV_HEAD
    rows = lax.broadcasted_iota(jnp.int32, (n2, n2), 0)
    cols_i = lax.broadcasted_iota(jnp.int32, (n2, n2), 1)
    strict = rows > cols_i
    incl = rows >= cols_i
    eye = (rows == cols_i).astype(F32)
    nt = (((1,), (1,)), ((), ()))
    tn = (((0,), (0,)), ((), ()))

    pairs = range(N_PAIRS)
    cs = [slice(p * PAIR, (p + 1) * PAIR) for p in pairs]
    bd_a = [_block_diag(at[:, c], first_head) for c in cs]
    bd_r = [_block_diag(rt[:, c], first_head) for c in cs]
    bd_b = [_block_diag(bt[:, c], first_head) for c in cs]
    bd_k = [_block_diag(kt[:, c], first_head) for c in cs]
    bd_bh = [_block_diag(bh[:, c], first_head) for c in cs]
    bd_kh = [_block_diag(kh[:, c], first_head) for c in cs]
    bd_v = [_block_diag(vb[:, c], first_head) for c in cs]

    gm = [lax.dot_general(jnp.concatenate([bd_a[p], bd_r[p]], axis=0),
                          jnp.concatenate([bd_b[p], bd_k[p]], axis=0),
                          nt, preferred_element_type=F32) for p in pairs]
    a_ab = [jnp.where(strict, gm[p][0:n2, 0:n2], 0.0) for p in pairs]
    a_ak = [jnp.where(strict, gm[p][0:n2, n2:2 * n2], 0.0).astype(BF16) for p in pairs]
    a_rb = [jnp.where(incl, gm[p][n2:2 * n2, 0:n2], 0.0).astype(BF16) for p in pairs]
    a_rk = [jnp.where(incl, gm[p][n2:2 * n2, n2:2 * n2], 0.0).astype(BF16) for p in pairs]

    tm = [eye + a_ab[p] for p in pairs]
    pb = [a_ab[p].astype(BF16) for p in pairs]
    pw = [_dot(pb[p], pb[p]) for p in pairs]
    for _ in range(4):
        pb = [pw[p].astype(BF16) for p in pairs]
        xm = [_dot(pb[p], jnp.concatenate([tm[p].astype(BF16), pb[p]], axis=1)) for p in pairs]
        tm = [tm[p] + xm[p][:, 0:n2] for p in pairs]
        pw = [xm[p][:, n2:2 * n2] for p in pairs]
    xm = [_dot(pw[p].astype(BF16), tm[p].astype(BF16)) for p in pairs]
    tm = [tm[p] + xm[p] for p in pairs]

    ta = [_dot(tm[p].astype(BF16), jnp.concatenate([bd_a[p], a_ak[p]], axis=1)).astype(BF16)
          for p in pairs]
    zp = [zst[p] for p in pairs]
    zv = [jnp.concatenate([zp[p].astype(BF16), bd_v[p]], axis=0) for p in pairs]
    ub = [_dot(ta[p], zv[p]).astype(BF16) for p in pairs]
    ym = [_dot(jnp.concatenate([bd_r[p], a_rk[p], a_rb[p]], axis=1),
               jnp.concatenate([zv[p], ub[p]], axis=0)) for p in pairs]
    zn = [lax.dot_general(jnp.concatenate([bd_bh[p], bd_kh[p]], axis=0),
                          jnp.concatenate([ub[p], bd_v[p]], axis=0),
                          tn, preferred_element_type=F32) for p in pairs]
    for p in pairs:
        pc_col = jnp.broadcast_to(pc[:, cs[p]], (n2, n2)).T
        zst[p] = pc_col * zp[p] + zn[p]
    ys = [ym[p][0:CHUNK, :] + ym[p][CHUNK:n2, :] for p in pairs]

    y = jnp.concatenate(ys, axis=1)
    inv_n = 1.0 / RWKV_HEAD
    mu = _head_sum(y, hsum) * inv_n
    dy = y - mu
    var = _head_sum(dy * dy, hsum) * inv_n
    yn = dy * lax.rsqrt(var + GN_EPS) * gg_ref[...] + gb_ref[...]
    o_ref[0] = (yn + bonus) * g


def _rwkv_branch(ur, mix, w0, w2, a0, a2, g2, k_k, k_a, r_k, gn_g, gn_b):
    bsz, t, _ = ur.shape
    dm = RWKV_DIM
    lora_w = DECAY_LORA + AAA_LORA
    w2p = jnp.concatenate([w2, jnp.zeros((AAA_LORA, dm), F32)], axis=0).astype(BF16)
    a2p = jnp.concatenate([jnp.zeros((DECAY_LORA, dm), F32), a2], axis=0).astype(BF16)
    hid = jnp.arange(HSUM_W) // RWKV_HEAD
    hsum = (hid[:, None] == hid[None, :]).astype(BF16)
    tid = jnp.arange(CHUNK)
    ltri = (tid[:, None] >= tid[None, :]).astype(BF16)
    row = lambda b, i: (b, i, 0)
    vec = lambda b, i: (0, 0)
    vspec = pl.BlockSpec((1, dm), vec)
    return pl.pallas_call(
        _rwkv_kernel,
        grid=(bsz, t // CHUNK),
        in_specs=[pl.BlockSpec((1, CHUNK, RWKV_COLS), row),
                  pl.BlockSpec((1, RWKV_COLS), vec),
                  vspec,
                  pl.BlockSpec((lora_w, dm), vec),
                  vspec,
                  pl.BlockSpec((lora_w, dm), vec),
                  pl.BlockSpec((GATE_LORA, dm), vec),
                  vspec, vspec, vspec, vspec, vspec,
                  pl.BlockSpec((HSUM_W, HSUM_W), vec),
                  pl.BlockSpec((CHUNK, CHUNK), vec)],
        out_specs=pl.BlockSpec((1, CHUNK, dm), row),
        out_shape=jax.ShapeDtypeStruct((bsz, t, dm), F32),
        scratch_shapes=[pltpu.VMEM((SUBLANES + CHUNK, RWKV_COLS), F32),
                        pltpu.VMEM((SUBLANES, RWKV_COLS), F32),
                        pltpu.VMEM((N_PAIRS, 2 * CHUNK, PAIR), F32)],
        compiler_params=pltpu.CompilerParams(dimension_semantics=("arbitrary", "arbitrary"),
                                             vmem_limit_bytes=VMEM_LIMIT),
        name="rwkv_branch",
    )(ur, mix.reshape(1, RWKV_COLS), w0.reshape(1, dm), w2p, a0.reshape(1, dm), a2p, g2.astype(BF16),
      k_k.reshape(1, dm), k_a.reshape(1, dm), r_k.reshape(1, dm), gn_g.reshape(1, dm), gn_b.reshape(1, dm),
      hsum, ltri)


def _post_kernel(x_ref, yc_ref, yr_ref, sg_ref, mod_ref, wo_ref, wout_ref, g2_ref, w1_ref, w2_ref, gf_ref,
                 o_ref):
    m = mod_ref[0]
    y_rwkv = _dot(yr_ref[0].astype(BF16), wo_ref[...])
    merged = yc_ref[0] + sg_ref[0] * y_rwkv
    x1 = x_ref[0] + m[2:3] * _dot(merged.astype(BF16), wout_ref[...])
    h2 = _rms_norm(x1, g2_ref[...]) * (1.0 + m[4:5]) + m[3:4]
    hid = jnp.maximum(_dot(h2.astype(BF16), w1_ref[...]), 0.0)
    ff = _dot((hid * hid).astype(BF16), w2_ref[...])
    x2 = x1 + m[5:6] * ff
    o_ref[0] = _rms_norm(x2, gf_ref[...])


def _post(x, yc, yr, sg, mod, w_o_bf, w_out_bf, g_norm2, w1_bf, w2_bf, g_out):
    bsz, t, d = x.shape
    tm = TM_POST
    row = lambda b, i: (b, i, 0)
    vec = lambda b, i: (0, 0)
    once = pl.Buffered(1)
    return pl.pallas_call(
        _post_kernel,
        grid=(bsz, t // tm),
        in_specs=[pl.BlockSpec((1, tm, d), row),
                  pl.BlockSpec((1, tm, d), row),
                  pl.BlockSpec((1, tm, d), row),
                  pl.BlockSpec((1, tm, d), lambda b, i: (b, i, 1)),
                  pl.BlockSpec((1, 6, d), lambda b, i: (b, 0, 0)),
                  pl.BlockSpec((d, d), vec, pipeline_mode=once),
                  pl.BlockSpec((d, d), vec, pipeline_mode=once),
                  pl.BlockSpec((1, d), vec),
                  pl.BlockSpec((d, D_FF), vec, pipeline_mode=once),
                  pl.BlockSpec((D_FF, d), vec, pipeline_mode=once),
                  pl.BlockSpec((1, d), vec)],
        out_specs=pl.BlockSpec((1, tm, d), row),
        out_shape=jax.ShapeDtypeStruct((bsz, t, d), F32),
        compiler_params=pltpu.CompilerParams(dimension_semantics=("arbitrary", "arbitrary"),
                                             vmem_limit_bytes=VMEM_LIMIT),
        name="post_ffn",
    )(x, yc, yr, sg, mod, w_o_bf, w_out_bf, g_norm2.reshape(1, d), w1_bf, w2_bf, g_out.reshape(1, d))


def kernel(x, c, w_ada, b_ada, g_norm1, w_in, conv_w, conv_b, conv_ln_g, conv_ln_b, w_conv_pw, b_conv_pw,
           rwkv_mix, rwkv_w0, rwkv_w2, rwkv_a0, rwkv_a2, rwkv_g2, rwkv_k_k, rwkv_k_a, rwkv_r_k, rwkv_gn_g,
           rwkv_gn_b, w_rwkv_o, w_out, g_norm2, w_ff1, w_ff2, g_final):
    assert w_ada.shape[0] == 1, "this problem's DEPTH is 1"
    bsz = x.shape[0]
    l = 0
    mod = _adaln(c, w_ada[l], b_ada[l]).reshape(bsz, 6, D_MODEL)
    z, ur, sg = _inproj(x, mod, g_norm1[l], w_in[l].astype(BF16))
    yc = _conv_branch(z, sg, conv_w[l], conv_b[l], conv_ln_g[l], conv_ln_b[l],
                      w_conv_pw[l].astype(BF16), b_conv_pw[l])
    yr = _rwkv_branch(ur, rwkv_mix[l], rwkv_w0[l], rwkv_w2[l], rwkv_a0[l], rwkv_a2[l], rwkv_g2[l],
                      rwkv_k_k[l], rwkv_k_a[l], rwkv_r_k[l], rwkv_gn_g[l], rwkv_gn_b[l])
    return _post(x, yc, yr, sg, mod, w_rwkv_o[l].astype(BF16), w_out[l].astype(BF16), g_norm2[l],
                 w_ff1[l].astype(BF16), w_ff2[l].astype(BF16), g_final)
```

```python
import functools

import jax
import jax.numpy as jnp
from jax import lax
from jax.experimental import pallas as pl
from jax.experimental.pallas import tpu as pltpu

F32 = jnp.float32
BF16 = jnp.bfloat16

D_MODEL = 1024
CONV_WIDTH = 31
CONV_DIM = D_MODEL
RWKV_DIM = D_MODEL
RWKV_HEAD = 64
DECAY_LORA = 64
AAA_LORA = 64
GATE_LORA = 128
D_FF = 4 * D_MODEL
RMS_EPS = 1e-6
LN_EPS = 1e-5
GN_EPS = 64e-5
RWKV_COLS = 3 * RWKV_DIM + DECAY_LORA + AAA_LORA + GATE_LORA
IN_COLS = 2 * CONV_DIM + RWKV_COLS + 2 * D_MODEL

LANES = 128
SUBLANES = 8
CHUNK = 64
PAIR = 2 * RWKV_HEAD
N_PAIRS = RWKV_DIM // PAIR
HSUM_W = 256
CONV_HALO = 32
CONV_ROWS = 64

TM_FRONT = 256
RWKV_ROWS = 16 * CHUNK
TM_POST = 512
VMEM_LIMIT = 56 * 1024 * 1024


def _dot(a, b):
    return jnp.dot(a, b, preferred_element_type=F32)


def _split2(x):
    hi = x.astype(BF16)
    lo = (x - hi.astype(F32)).astype(BF16)
    return hi, lo


def _split3(x):
    hi = x.astype(BF16)
    r1 = x - hi.astype(F32)
    mid = r1.astype(BF16)
    lo = (r1 - mid.astype(F32)).astype(BF16)
    return hi, mid, lo


def _sigmoid(x):
    return jax.nn.sigmoid(x)


def _rms_norm(x, gain):
    ms = jnp.mean(x * x, axis=-1, keepdims=True)
    return x * lax.rsqrt(ms + RMS_EPS) * gain


def _adaln_kernel(c_ref, w_ref, b_ref, o_ref):
    cv = c_ref[...]
    s = cv * _sigmoid(cv)
    s_hi, s_lo = _split2(s)
    w_hi, w_lo = _split2(w_ref[...])
    o_ref[...] = _dot(s_hi, w_hi) + _dot(s_hi, w_lo) + _dot(s_lo, w_hi) + b_ref[...]


def _adaln(c, w_ada, b_ada):
    bsz, d = c.shape
    n = w_ada.shape[1]
    tn = D_MODEL
    return pl.pallas_call(
        _adaln_kernel,
        grid=(n // tn,),
        in_specs=[pl.BlockSpec((bsz, d), lambda j: (0, 0)),
                  pl.BlockSpec((d, tn), lambda j: (0, j)),
                  pl.BlockSpec((1, tn), lambda j: (0, j))],
        out_specs=pl.BlockSpec((bsz, tn), lambda j: (0, j)),
        out_shape=jax.ShapeDtypeStruct((bsz, n), F32),
        compiler_params=pltpu.CompilerParams(dimension_semantics=("arbitrary",)),
        name="adaln",
    )(c, w_ada, b_ada.reshape(1, n))


def _front_kernel(x_ref, mod_ref, g1_ref, w_ref, cw_ref, cb_ref, lg_ref, lb_ref, wpw_ref, bpw_ref, mix_ref,
                  yc_ref, xs_ref, sg_ref, zbuf, cbuf, hbuf, gbuf, ustage, ucarry):
    tt = x_ref.shape[1]
    ti = pl.program_id(1)
    n_cb = CONV_DIM // LANES
    shifted_rows = tt + CONV_HALO - SUBLANES
    first = CONV_HALO - (CONV_WIDTH - 1)

    @pl.when(ti == 0)
    def _():
        zbuf[0, :, 0:CONV_HALO, :] = jnp.zeros((n_cb, CONV_HALO, LANES), F32)
        ucarry[...] = jnp.zeros_like(ucarry)

    @pl.when(ti > 0)
    def _():
        zbuf[0, :, 0:CONV_HALO, :] = zbuf[0, :, tt:tt + CONV_HALO, :]

    x = x_ref[0]
    m = mod_ref[0]
    h = _rms_norm(x, g1_ref[...])
    h = h * (1.0 + m[1:2]) + m[0:1]
    hbuf[...] = h.astype(BF16)

    mxu_w = 2 * LANES
    rest0 = 2 * CONV_DIM
    gate0 = rest0 + RWKV_COLS

    def glu_piece(j):
        za = _dot(hbuf[...], w_ref[:, j * mxu_w:(j + 1) * mxu_w])
        zb = _dot(hbuf[...], w_ref[:, CONV_DIM + j * mxu_w:CONV_DIM + (j + 1) * mxu_w])
        z = za * _sigmoid(zb)
        for i in range(mxu_w // LANES):
            zbuf[0, j * mxu_w // LANES + i, CONV_HALO:CONV_HALO + tt, :] = z[:, i * LANES:(i + 1) * LANES]

    def rest_piece(c0):
        res = _dot(hbuf[...], w_ref[:, c0:c0 + mxu_w])
        if c0 < gate0:
            cols = slice(c0 - rest0, c0 - rest0 + mxu_w)
            pi = (c0 - rest0) // mxu_w
            ustage[pi, 0:SUBLANES, :] = ucarry[:, cols]
            ustage[pi, SUBLANES:SUBLANES + tt, :] = res
            ucarry[:, cols] = res[tt - SUBLANES:tt, :]
            prev = ustage[pi, SUBLANES - 1:SUBLANES - 1 + tt, :]
            xs_ref[0, :, cols] = res + mix_ref[:, cols] * (prev - res)
        elif c0 < gate0 + D_MODEL:
            gbuf[:, c0 - gate0:c0 - gate0 + mxu_w] = _sigmoid(res)
        else:
            d0 = gate0 + D_MODEL
            sg_ref[0, :, c0 - d0:c0 - d0 + mxu_w] = _sigmoid(res)

    def shift_unit(cb):
        for s in range(1, SUBLANES):
            zbuf[s, cb, 0:shifted_rows, :] = zbuf[0, cb, s:s + shifted_rows, :]

    def conv_unit(cb, r0):
        cols = slice(cb * LANES, (cb + 1) * LANES)
        acc = jnp.broadcast_to(cb_ref[:, cols], (CONV_ROWS, LANES))
        for s in range(SUBLANES):
            taps = [k for k in range(CONV_WIDTH) if (first + k) % SUBLANES == s]
            a_max = max((first + k) // SUBLANES for k in taps)
            win = zbuf[s, cb, r0:r0 + a_max * SUBLANES + CONV_ROWS, :]
            for k in taps:
                a = (first + k) // SUBLANES
                acc = acc + cw_ref[k:k + 1, cols] * win[a * SUBLANES:a * SUBLANES + CONV_ROWS, :]
        cbuf[r0:r0 + CONV_ROWS, cols] = acc

    mxu_q = [(functools.partial(glu_piece, j), 2.0, j) for j in range(CONV_DIM // mxu_w)]
    mxu_q += [(functools.partial(rest_piece, c0), 1.0, None) for c0 in range(rest0, IN_COLS, mxu_w)]
    vec_q = []
    for cb in range(n_cb):
        vec_q.append((functools.partial(shift_unit, cb), 0.6, cb * LANES // mxu_w))
        vec_q += [(functools.partial(conv_unit, cb, r0), 0.6, None) for r0 in range(0, tt, CONV_ROWS)]
    glu_done, t_mxu, t_vec = -1, 0.0, 0.0
    while mxu_q or vec_q:
        need = vec_q[0][2] if vec_q and vec_q[0][2] is not None else -1
        if mxu_q and (not vec_q or need > glu_done or t_mxu <= t_vec):
            fn, cost, j = mxu_q.pop(0)
            glu_done = j if j is not None else glu_done
            t_mxu += cost
        else:
            fn, cost, _ = vec_q.pop(0)
            t_vec += cost
        fn()

    cv = cbuf[...]
    mu = jnp.mean(cv, axis=-1, keepdims=True)
    dv = cv - mu
    var = jnp.mean(dv * dv, axis=-1, keepdims=True)
    yl = dv * lax.rsqrt(var + LN_EPS) * lg_ref[...] + lb_ref[...]
    sw = yl * _sigmoid(yl)
    y = _dot(sw.astype(BF16), wpw_ref[...]) + bpw_ref[...]
    yc_ref[0] = gbuf[...] * y


def _front(x, mod, g_norm1, w_in_bf, conv_w, conv_b, ln_g, ln_b, w_pw_bf, b_pw, mix):
    bsz, t, d = x.shape
    cd = CONV_DIM
    tm = TM_FRONT
    row = lambda b, i: (b, i, 0)
    vec = lambda b, i: (0, 0)
    once = pl.Buffered(1)
    return pl.pallas_call(
        _front_kernel,
        grid=(bsz, t // tm),
        in_specs=[pl.BlockSpec((1, tm, d), row),
                  pl.BlockSpec((1, 6, d), lambda b, i: (b, 0, 0)),
                  pl.BlockSpec((1, d), vec),
                  pl.BlockSpec((d, IN_COLS), vec, pipeline_mode=once),
                  pl.BlockSpec((CONV_WIDTH, cd), vec),
                  pl.BlockSpec((1, cd), vec),
                  pl.BlockSpec((1, cd), vec),
                  pl.BlockSpec((1, cd), vec),
                  pl.BlockSpec((cd, d), vec, pipeline_mode=once),
                  pl.BlockSpec((1, d), vec),
                  pl.BlockSpec((1, RWKV_COLS), vec)],
        out_specs=[pl.BlockSpec((1, tm, d), row),
                   pl.BlockSpec((1, tm, RWKV_COLS), row),
                   pl.BlockSpec((1, tm, d), row)],
        out_shape=[jax.ShapeDtypeStruct((bsz, t, d), F32),
                   jax.ShapeDtypeStruct((bsz, t, RWKV_COLS), F32),
                   jax.ShapeDtypeStruct((bsz, t, d), F32)],
        scratch_shapes=[pltpu.VMEM((SUBLANES, cd // LANES, tm + CONV_HALO, LANES), F32),
                        pltpu.VMEM((tm, cd), F32),
                        pltpu.VMEM((tm, d), BF16),
                        pltpu.VMEM((tm, d), F32),
                        pltpu.VMEM((RWKV_COLS // (2 * LANES), SUBLANES + tm, 2 * LANES), F32),
                        pltpu.VMEM((SUBLANES, RWKV_COLS), F32)],
        compiler_params=pltpu.CompilerParams(dimension_semantics=("arbitrary", "arbitrary"),
                                             vmem_limit_bytes=VMEM_LIMIT),
        name="front",
    )(x, mod, g_norm1.reshape(1, d), w_in_bf, conv_w, conv_b.reshape(1, cd), ln_g.reshape(1, cd),
      ln_b.reshape(1, cd), w_pw_bf, b_pw.reshape(1, d), mix.reshape(1, RWKV_COLS))


def _head_sum(x, hsum):
    outs = [_dot(x[:, q * HSUM_W:(q + 1) * HSUM_W].astype(BF16), hsum) for q in range(x.shape[1] // HSUM_W)]
    return jnp.concatenate(outs, axis=1)


def _block_diag(xp, first_head):
    zero = jnp.zeros_like(xp)
    return jnp.concatenate([jnp.where(first_head, xp, zero), jnp.where(first_head, zero, xp)], axis=0)


def _rwkv_kernel(xs_ref, w0_ref, w2_ref, a0_ref, a2_ref, g2_ref, kk_ref, ka_ref, rk_ref,
                 gg_ref, gb_ref, hsum_ref, ltri_ref, o_ref, zst, opb, opf, pcb):
    si = pl.program_id(1)
    n_sub = xs_ref.shape[1] // CHUNK

    @pl.when(si == 0)
    def _():
        zst[...] = jnp.zeros_like(zst)

    prep_refs = (xs_ref, w0_ref, w2_ref, a0_ref, a2_ref, g2_ref, kk_ref, ka_ref, rk_ref, hsum_ref, ltri_ref,
                 opb, opf, pcb)
    scan_refs = (gg_ref, gb_ref, hsum_ref, o_ref, zst, opb, opf, pcb)

    for _ in _rwkv_prep(0, prep_refs):
        pass

    def body(c, carry_val):
        scan = _rwkv_scan(pl.multiple_of(c * CHUNK, CHUNK), scan_refs)
        prep = _rwkv_prep(pl.multiple_of(jnp.minimum(c + 1, n_sub - 1) * CHUNK, CHUNK), prep_refs)
        for _ in range(3):
            next(prep)
        live = [scan, prep]
        while live:
            for gen in list(live):
                if next(gen, _DONE) is _DONE:
                    live.remove(gen)
        return carry_val

    lax.fori_loop(0, n_sub, body, 0)


_DONE = object()


def _rwkv_prep(r0, refs):
    (xs_ref, w0_ref, w2_ref, a0_ref, a2_ref, g2_ref, kk_ref, ka_ref, rk_ref, hsum_ref, ltri_ref,
     opb, opf, pcb) = refs
    xs = xs_ref[0, pl.ds(r0, CHUNK), :]
    yield

    r = xs[:, 0:RWKV_DIM]
    k = xs[:, RWKV_DIM:2 * RWKV_DIM]
    v = xs[:, 2 * RWKV_DIM:3 * RWKV_DIM]
    xwa = xs[:, 3 * RWKV_DIM:3 * RWKV_DIM + DECAY_LORA + AAA_LORA]
    xg = xs[:, 3 * RWKV_DIM + DECAY_LORA + AAA_LORA:RWKV_COLS]

    txw = jnp.tanh(xwa).astype(BF16)
    sxg = _sigmoid(xg).astype(BF16)
    kraw = k * kk_ref[...]
    yield

    q = w0_ref[...] + _dot(txw, w2_ref[...])
    ta2 = _dot(xwa.astype(BF16), a2_ref[...])
    g = _dot(sxg, g2_ref[...])
    hsum = hsum_ref[...]
    kn2 = _head_sum(kraw * kraw, hsum)
    yield

    w_log = -(jnp.maximum(-q, 0.0) + jnp.log1p(jnp.exp(-jnp.abs(q)))) - 0.5
    lw = -jnp.exp(w_log)
    asig = _sigmoid(a0_ref[...] + ta2)
    kk = kraw / jnp.maximum(jnp.sqrt(kn2), 1e-12)
    k2 = k * (1.0 + (asig - 1.0) * ka_ref[...])
    l_hi, l_mid, l_lo = _split3(lw)
    yield

    ltri = ltri_ref[...]
    cl = _dot(ltri, l_hi) + _dot(ltri, l_mid) + _dot(ltri, l_lo)
    bonus = _head_sum(r * k2 * rk_ref[...], hsum) * v
    yield

    e_in = jnp.exp(cl)
    e_ex = jnp.exp(cl - lw)
    e_neg = jnp.exp(-cl)
    pc = e_in[CHUNK - 1:CHUNK, :]
    yield

    bt_f = kk * asig * e_neg
    kt_f = k2 * e_neg
    opf[0] = bonus
    opf[1] = g
    opb[0] = (r * e_in).astype(BF16)
    opb[1] = (-kk * e_ex).astype(BF16)
    opb[2] = bt_f.astype(BF16)
    opb[3] = kt_f.astype(BF16)
    opb[4] = (bt_f * pc).astype(BF16)
    opb[5] = (kt_f * pc).astype(BF16)
    opb[6] = v.astype(BF16)
    pcb[...] = jnp.broadcast_to(pc, pcb.shape)
    yield


def _rwkv_scan(r0, refs):
    gg_ref, gb_ref, hsum_ref, o_ref, zst, opb, opf, pcb = refs
    rt, at, bt, kt, bh, kh, vb = (opb[i] for i in range(7))
    bonus = opf[0]
    g = opf[1]
    pc = pcb[0:1, :]
    hsum = hsum_ref[...]

    n2 = 2 * CHUNK
    lane = lax.broadcasted_iota(jnp.int32, (CHUNK, PAIR), 1)
    first_head = lane < RWKV_HEAD
    rows = lax.broadcasted_iota(jnp.int32, (n2, n2), 0)
    cols_i = lax.broadcasted_iota(jnp.int32, (n2, n2), 1)
    strict = rows > cols_i
    incl = rows >= cols_i
    eye = (rows == cols_i).astype(F32)
    nt = (((1,), (1,)), ((), ()))
    tn = (((0,), (0,)), ((), ()))

    pairs = range(N_PAIRS)
    cs = [slice(p * PAIR, (p + 1) * PAIR) for p in pairs]
    bd_a = [_block_diag(at[:, c], first_head) for c in cs]
    bd_r = [_block_diag(rt[:, c], first_head) for c in cs]
    bd_b = [_block_diag(bt[:, c], first_head) for c in cs]
    bd_k = [_block_diag(kt[:, c], first_head) for c in cs]
    bd_bh = [_block_diag(bh[:, c], first_head) for c in cs]
    bd_kh = [_block_diag(kh[:, c], first_head) for c in cs]
    bd_v = [_block_diag(vb[:, c], first_head) for c in cs]

    gm = [lax.dot_general(jnp.concatenate([bd_a[p], bd_r[p]], axis=0),
                          jnp.concatenate([bd_b[p], bd_k[p]], axis=0),
                          nt, preferred_element_type=F32) for p in pairs]
    a_ab = [jnp.where(strict, gm[p][0:n2, 0:n2], 0.0) for p in pairs]
    a_ak = [jnp.where(strict, gm[p][0:n2, n2:2 * n2], 0.0).astype(BF16) for p in pairs]
    a_rb = [jnp.where(incl, gm[p][n2:2 * n2, 0:n2], 0.0).astype(BF16) for p in pairs]
    a_rk = [jnp.where(incl, gm[p][n2:2 * n2, n2:2 * n2], 0.0).astype(BF16) for p in pairs]
    yield

    tm = [eye + a_ab[p] for p in pairs]
    pb = [a_ab[p].astype(BF16) for p in pairs]
    pw = [_dot(pb[p], pb[p]) for p in pairs]
    yield
    for _ in range(4):
        pb = [pw[p].astype(BF16) for p in pairs]
        xm = [_dot(pb[p], jnp.concatenate([tm[p].astype(BF16), pb[p]], axis=1)) for p in pairs]
        tm = [tm[p] + xm[p][:, 0:n2] for p in pairs]
        pw = [xm[p][:, n2:2 * n2] for p in pairs]
        yield
    xm = [_dot(pw[p].astype(BF16), tm[p].astype(BF16)) for p in pairs]
    tm = [tm[p] + xm[p] for p in pairs]

    ta = [_dot(tm[p].astype(BF16), jnp.concatenate([bd_a[p], a_ak[p]], axis=1)).astype(BF16)
          for p in pairs]
    zp = [zst[p] for p in pairs]
    zv = [jnp.concatenate([zp[p].astype(BF16), bd_v[p]], axis=0) for p in pairs]
    ub = [_dot(ta[p], zv[p]).astype(BF16) for p in pairs]
    ym = [_dot(jnp.concatenate([bd_r[p], a_rk[p], a_rb[p]], axis=1),
               jnp.concatenate([zv[p], ub[p]], axis=0)) for p in pairs]
    zn = [lax.dot_general(jnp.concatenate([bd_bh[p], bd_kh[p]], axis=0),
                          jnp.concatenate([ub[p], bd_v[p]], axis=0),
                          tn, preferred_element_type=F32) for p in pairs]
    for p in pairs:
        pc_col = jnp.broadcast_to(pc[:, cs[p]], (n2, n2)).T
        zst[p] = pc_col * zp[p] + zn[p]
    ys = [ym[p][0:CHUNK, :] + ym[p][CHUNK:n2, :] for p in pairs]

    y = jnp.concatenate(ys, axis=1)
    inv_n = 1.0 / RWKV_HEAD
    mu = _head_sum(y, hsum) * inv_n
    dy = y - mu
    var = _head_sum(dy * dy, hsum) * inv_n
    yn = dy * lax.rsqrt(var + GN_EPS) * gg_ref[...] + gb_ref[...]
    o_ref[0, pl.ds(r0, CHUNK), :] = (yn + bonus) * g
    yield


def _rwkv_branch(xs, w0, w2, a0, a2, g2, k_k, k_a, r_k, gn_g, gn_b):
    bsz, t, _ = xs.shape
    dm = RWKV_DIM
    lora_w = DECAY_LORA + AAA_LORA
    w2p = jnp.concatenate([w2, jnp.zeros((AAA_LORA, dm), F32)], axis=0).astype(BF16)
    a2p = jnp.concatenate([jnp.zeros((DECAY_LORA, dm), F32), a2], axis=0).astype(BF16)
    hid = jnp.arange(HSUM_W) // RWKV_HEAD
    hsum = (hid[:, None] == hid[None, :]).astype(BF16)
    tid = jnp.arange(CHUNK)
    ltri = (tid[:, None] >= tid[None, :]).astype(BF16)
    row = lambda b, i: (b, i, 0)
    vec = lambda b, i: (0, 0)
    vspec = pl.BlockSpec((1, dm), vec)
    return pl.pallas_call(
        _rwkv_kernel,
        grid=(bsz, t // RWKV_ROWS),
        in_specs=[pl.BlockSpec((1, RWKV_ROWS, RWKV_COLS), row),
                  vspec,
                  pl.BlockSpec((lora_w, dm), vec),
                  vspec,
                  pl.BlockSpec((lora_w, dm), vec),
                  pl.BlockSpec((GATE_LORA, dm), vec),
                  vspec, vspec, vspec, vspec, vspec,
                  pl.BlockSpec((HSUM_W, HSUM_W), vec),
                  pl.BlockSpec((CHUNK, CHUNK), vec)],
        out_specs=pl.BlockSpec((1, RWKV_ROWS, dm), row),
        out_shape=jax.ShapeDtypeStruct((bsz, t, dm), F32),
        scratch_shapes=[pltpu.VMEM((N_PAIRS, 2 * CHUNK, PAIR), F32),
                        pltpu.VMEM((7, CHUNK, dm), BF16),
                        pltpu.VMEM((2, CHUNK, dm), F32),
                        pltpu.VMEM((SUBLANES, dm), F32)],
        compiler_params=pltpu.CompilerParams(dimension_semantics=("arbitrary", "arbitrary"),
                                             vmem_limit_bytes=VMEM_LIMIT),
        name="rwkv_branch",
    )(xs, w0.reshape(1, dm), w2p, a0.reshape(1, dm), a2p, g2.astype(BF16),
      k_k.reshape(1, dm), k_a.reshape(1, dm), r_k.reshape(1, dm), gn_g.reshape(1, dm), gn_b.reshape(1, dm),
      hsum, ltri)


def _post_kernel(x_ref, yc_ref, yr_ref, sg_ref, mod_ref, wo_ref, wout_ref, g2_ref, w1_ref, w2_ref, gf_ref,
                 o_ref):
    m = mod_ref[0]
    y_rwkv = _dot(yr_ref[0].astype(BF16), wo_ref[...])
    merged = yc_ref[0] + sg_ref[0] * y_rwkv
    x1 = x_ref[0] + m[2:3] * _dot(merged.astype(BF16), wout_ref[...])
    h2 = _rms_norm(x1, g2_ref[...]) * (1.0 + m[4:5]) + m[3:4]
    hid = jnp.maximum(_dot(h2.astype(BF16), w1_ref[...]), 0.0)
    ff = _dot((hid * hid).astype(BF16), w2_ref[...])
    x2 = x1 + m[5:6] * ff
    o_ref[0] = _rms_norm(x2, gf_ref[...])


def _post(x, yc, yr, sg, mod, w_o_bf, w_out_bf, g_norm2, w1_bf, w2_bf, g_out):
    bsz, t, d = x.shape
    tm = TM_POST
    row = lambda b, i: (b, i, 0)
    vec = lambda b, i: (0, 0)
    once = pl.Buffered(1)
    return pl.pallas_call(
        _post_kernel,
        grid=(bsz, t // tm),
        in_specs=[pl.BlockSpec((1, tm, d), row),
                  pl.BlockSpec((1, tm, d), row),
                  pl.BlockSpec((1, tm, d), row),
                  pl.BlockSpec((1, tm, d), row),
                  pl.BlockSpec((1, 6, d), lambda b, i: (b, 0, 0)),
                  pl.BlockSpec((d, d), vec, pipeline_mode=once),
                  pl.BlockSpec((d, d), vec, pipeline_mode=once),
                  pl.BlockSpec((1, d), vec),
                  pl.BlockSpec((d, D_FF), vec, pipeline_mode=once),
                  pl.BlockSpec((D_FF, d), vec, pipeline_mode=once),
                  pl.BlockSpec((1, d), vec)],
        out_specs=pl.BlockSpec((1, tm, d), row),
        out_shape=jax.ShapeDtypeStruct((bsz, t, d), F32),
        compiler_params=pltpu.CompilerParams(dimension_semantics=("arbitrary", "arbitrary"),
                                             vmem_limit_bytes=VMEM_LIMIT),
        name="post_ffn",
    )(x, yc, yr, sg, mod, w_o_bf, w_out_bf, g_norm2.reshape(1, d), w1_bf, w2_bf, g_out.reshape(1, d))


def kernel(x, c, w_ada, b_ada, g_norm1, w_in, conv_w, conv_b, conv_ln_g, conv_ln_b, w_conv_pw, b_conv_pw,
           rwkv_mix, rwkv_w0, rwkv_w2, rwkv_a0, rwkv_a2, rwkv_g2, rwkv_k_k, rwkv_k_a, rwkv_r_k, rwkv_gn_g,
           rwkv_gn_b, w_rwkv_o, w_out, g_norm2, w_ff1, w_ff2, g_final):
    assert w_ada.shape[0] == 1, "this problem's DEPTH is 1"
    bsz = x.shape[0]
    l = 0
    mod = _adaln(c, w_ada[l], b_ada[l]).reshape(bsz, 6, D_MODEL)
    yc, xs, sg = _front(x, mod, g_norm1[l], w_in[l].astype(BF16), conv_w[l], conv_b[l], conv_ln_g[l],
                        conv_ln_b[l], w_conv_pw[l].astype(BF16), b_conv_pw[l], rwkv_mix[l])
    yr = _rwkv_branch(xs, rwkv_w0[l], rwkv_w2[l], rwkv_a0[l], rwkv_a2[l], rwkv_g2[l],
                      rwkv_k_k[l], rwkv_k_a[l], rwkv_r_k[l], rwkv_gn_g[l], rwkv_gn_b[l])
    return _post(x, yc, yr, sg, mod, w_rwkv_o[l].astype(BF16), w_out[l].astype(BF16), g_norm2[l],
                 w_ff1[l].astype(BF16), w_ff2[l].astype(BF16), g_final)
```
